```python
import math
import jax
import jax.numpy as jnp
from jax import lax
import numpy as np

D_MODEL = 1024
BATCH = 4
SEQ = 4096
DEPTH = 1

GRID_W = 64
CTX_LEN = 256
EPS = 1e-6
SSM_WIDTH = 512
SSM_GROUP = 16
SSM_GROUPS = SSM_WIDTH // SSM_GROUP
SSM_STATE = 64
DT_MIN = 1e-3
DT_MAX = 1e-1
MLA_HEADS = 8
QK_NOPE = 64
QK_ROPE = 32
QK_DIM = QK_NOPE + QK_ROPE
V_DIM = 64
Q_LORA = 384
KV_LORA = 256
ROPE_THETA = 10000.0
Q_BLOCK = 128
IN_SPLITS = (SSM_WIDTH, SSM_WIDTH + Q_LORA, SSM_WIDTH + Q_LORA + KV_LORA, SSM_WIDTH + Q_LORA + KV_LORA + QK_ROPE)
IN_COLS = IN_SPLITS[-1] + 2 * D_MODEL
N_EXPERTS = 16
EXPERT_FF = 2816
CAPACITY_FACTOR = 2

kernel_name = "hybrid_s5_mla_ec_moe_diffusion_block"


def rmsnorm(x, g):
    xf = x.astype(jnp.float32)
    y = xf * lax.rsqrt(jnp.mean(xf * xf, axis=-1, keepdims=True) + EPS)
    return (y * g.astype(jnp.float32)).astype(x.dtype)


def modulate(h, shift, scale):
    return h * (1 + scale) + shift


def axial_rope(n):
    rows = n // GRID_W
    row = jnp.repeat(jnp.arange(rows, dtype=jnp.float32), GRID_W)
    col = jnp.tile(jnp.arange(GRID_W, dtype=jnp.float32), rows)
    pairs = QK_ROPE // 4
    inv_freq = ROPE_THETA ** (-jnp.arange(pairs, dtype=jnp.float32) / pairs)
    ang = jnp.stack([row[:, None] * inv_freq, col[:, None] * inv_freq], axis=1)
    return jnp.cos(ang), jnp.sin(ang)


def apply_rope(x, cos, sin):
    xf = x.astype(jnp.float32).reshape(x.shape[:-1] + (2, QK_ROPE // 2))
    x1, x2 = jnp.split(xf, 2, axis=-1)
    cs, sn = cos[None, :, None], sin[None, :, None]
    out = jnp.concatenate([x1 * cs - x2 * sn, x2 * cs + x1 * sn], axis=-1)
    return out.reshape(x.shape).astype(x.dtype)


def mla_q(q_a, q_a_g, w_qb, q_norm_g, rope):
    b, n, _ = q_a.shape
    q = (rmsnorm(q_a, q_a_g) @ w_qb).reshape(b, n, MLA_HEADS, QK_DIM)
    q = rmsnorm(q, q_norm_g)
    if rope is not None:
        q = jnp.concatenate([q[..., :QK_NOPE], apply_rope(q[..., QK_NOPE:], *rope)], axis=-1)
    return q


def mla_kv(kv_a, k_r, kv_a_g, w_kvb, k_norm_g, rope):
    b, n, _ = kv_a.shape
    kv = (rmsnorm(kv_a, kv_a_g) @ w_kvb).reshape(b, n, MLA_HEADS, QK_NOPE + V_DIM)
    k_nope, v = jnp.split(kv, [QK_NOPE], axis=-1)
    k_rope = jnp.broadcast_to(k_r[:, :, None, :], (b, n, MLA_HEADS, QK_ROPE))
    k = rmsnorm(jnp.concatenate([k_nope, k_rope], axis=-1), k_norm_g)
    if rope is not None:
        k = jnp.concatenate([k[..., :QK_NOPE], apply_rope(k[..., QK_NOPE:], *rope)], axis=-1)
    return k, v


def attend(q, k, v):
    s = jnp.einsum('bqhd,bkhd->bhqk', q, k).astype(jnp.float32) * (QK_DIM ** -0.5)
    p = jax.nn.softmax(s, axis=-1).astype(v.dtype)
    out = jnp.einsum('bhqk,bkhd->bqhd', p, v)
    return out.reshape(out.shape[:2] + (MLA_HEADS * V_DIM,))


def attend_blocked(q, k, v):
    b, n = q.shape[:2]
    qb = q.reshape(b, n // Q_BLOCK, Q_BLOCK, MLA_HEADS, QK_DIM).transpose(1, 0, 2, 3, 4)
    out = lax.map(lambda qi: attend(qi, k, v), qb)
    return out.transpose(1, 0, 2, 3).reshape(b, n, MLA_HEADS * V_DIM)


def s5_discretise(lam_re, lam_im, log_dt, b_re, b_im):
    lam = lax.complex(lam_re.astype(jnp.float32), lam_im.astype(jnp.float32))
    dt = jnp.exp(log_dt.astype(jnp.float32))[:, None]
    lam_bar = jnp.exp(lam * dt)
    b_bar = ((lam_bar - 1) / lam)[..., None] * lax.complex(b_re.astype(jnp.float32), b_im.astype(jnp.float32))
    return lam_bar, b_bar


def s5_drive(u, b_bar):
    b, n, _ = u.shape
    ug = u.reshape(b, n, SSM_GROUPS, SSM_GROUP).astype(jnp.complex64)
    return jnp.einsum('bngh,gph->bngp', ug, b_bar)


def _linrec(e1, e2):
    a1, b1 = e1
    a2, b2 = e2
    return a1 * a2, a2 * b1 + b2


def s5_scan(bu, lam_bar, reverse, h0=None):
    n = bu.shape[1]
    a = jnp.broadcast_to(lam_bar, (1, n) + lam_bar.shape)
    a_cum, xs = lax.associative_scan(_linrec, (a, bu), reverse=reverse, axis=1)
    if h0 is not None:
        xs = xs + a_cum * h0[:, None]
    return xs


def s5_readout(xs, c_out):
    b, n = xs.shape[:2]
    return jnp.einsum('bngp,ghp->bngh', xs, c_out).real.reshape(b, n, SSM_WIDTH)


def s5_branch_out(y, w_glu, b_glu, w_ssm_o):
    y = jax.nn.gelu(y)
    y = y * jax.nn.sigmoid(y @ w_glu + b_glu)
    return y @ w_ssm_o


def merge_branches(ssm_out, attn_out, gates, w_out):
    g_ssm, g_attn = jnp.split(gates, 2, axis=-1)
    return (jax.nn.sigmoid(g_ssm) * ssm_out + jax.nn.sigmoid(g_attn) * attn_out) @ w_out


def expert_choice_ffn(h, w_router, w_gate, w_up, w_down):
    b, n, _ = h.shape
    cap = CAPACITY_FACTOR * n // N_EXPERTS
    aff = jax.nn.softmax((h @ w_router).astype(jnp.float32), axis=-1)
    gate, idx = lax.top_k(aff.transpose(0, 2, 1), cap)
    bidx = jnp.arange(b)[:, None, None]
    xs = h[bidx, idx]
    hid = jax.nn.silu(jnp.einsum('becd,edf->becf', xs, w_gate)) * jnp.einsum('becd,edf->becf', xs, w_up)
    ys = jnp.einsum('becf,efd->becd', hid, w_down) * gate[..., None].astype(h.dtype)
    return jnp.zeros_like(h).at[bidx, idx].add(ys)


def setup_inputs(seed: int = 0) -> dict:
    key = jax.random.key(seed)
    ks = iter(jax.random.split(key, 40))
    f32 = jnp.float32

    def nrm(shape, scale):
        return scale * jax.random.normal(next(ks), shape, f32)

    L, G, P, H = DEPTH, SSM_GROUPS, SSM_STATE, MLA_HEADS
    x = nrm((BATCH, SEQ, D_MODEL), 1.0)
    c = nrm((BATCH, D_MODEL), 1.0)
    ctx = nrm((BATCH, CTX_LEN, D_MODEL), 1.0)
    c_ctx = nrm((D_MODEL,), 1.0)
    w_ada = nrm((L, D_MODEL, 6 * D_MODEL), 0.5 * D_MODEL ** -0.5)
    b_ada = nrm((L, 6 * D_MODEL), 0.02)
    norm1_g = 1.0 + nrm((L, D_MODEL), 0.02)
    norm2_g = 1.0 + nrm((L, D_MODEL), 0.02)
    w_in = nrm((L, D_MODEL, IN_COLS), D_MODEL ** -0.5)
    q_a_g = 1.0 + nrm((L, Q_LORA), 0.02)
    w_qb = nrm((L, Q_LORA, H * QK_DIM), Q_LORA ** -0.5)
    kv_a_g = 1.0 + nrm((L, KV_LORA), 0.02)
    w_kvb = nrm((L, KV_LORA, H * (QK_NOPE + V_DIM)), KV_LORA ** -0.5)
    q_norm_g = 1.0 + nrm((L, QK_DIM), 0.02)
    k_norm_g = 1.0 + nrm((L, QK_DIM), 0.02)
    w_mla_o = nrm((L, H * V_DIM, D_MODEL), (H * V_DIM) ** -0.5)
    ssm_lam_re = -0.5 + nrm((L, 2, G, P), 0.01)
    ssm_lam_im = jnp.pi * jnp.arange(P, dtype=f32) + nrm((L, 2, G, P), 0.01)
    ssm_log_dt = jax.random.uniform(next(ks), (L, 2, G), f32, math.log(DT_MIN), math.log(DT_MAX))
    ssm_b_re = nrm((L, 2, G, P, SSM_GROUP), (2 * SSM_GROUP) ** -0.5)
    ssm_b_im = nrm((L, 2, G, P, SSM_GROUP), (2 * SSM_GROUP) ** -0.5)
    ssm_c_re = nrm((L, 2, G, SSM_GROUP, P), P ** -0.5)
    ssm_c_im = nrm((L, 2, G, SSM_GROUP, P), P ** -0.5)
    ssm_d = nrm((L, SSM_WIDTH), 1.0)
    w_glu = nrm((L, SSM_WIDTH, SSM_WIDTH), SSM_WIDTH ** -0.5)
    b_glu = nrm((L, SSM_WIDTH), 0.02)
    w_ssm_o = nrm((L, SSM_WIDTH, D_MODEL), SSM_WIDTH ** -0.5)
    w_out = nrm((L, D_MODEL, D_MODEL), D_MODEL ** -0.5)
    w_router = nrm((L, D_MODEL, N_EXPERTS), D_MODEL ** -0.5)
    w_e_gate = nrm((L, N_EXPERTS, D_MODEL, EXPERT_FF), D_MODEL ** -0.5)
    w_e_up = nrm((L, N_EXPERTS, D_MODEL, EXPERT_FF), D_MODEL ** -0.5)
    w_e_down = nrm((L, N_EXPERTS, EXPERT_FF, D_MODEL), EXPERT_FF ** -0.5)
    return {"x": x, "c": c, "ctx": ctx, "c_ctx": c_ctx, "w_ada": w_ada, "b_ada": b_ada,
            "norm1_g": norm1_g, "norm2_g": norm2_g, "w_in": w_in, "q_a_g": q_a_g, "w_qb": w_qb,
            "kv_a_g": kv_a_g, "w_kvb": w_kvb, "q_norm_g": q_norm_g, "k_norm_g": k_norm_g,
            "w_mla_o": w_mla_o, "ssm_lam_re": ssm_lam_re, "ssm_lam_im": ssm_lam_im,
            "ssm_log_dt": ssm_log_dt, "ssm_b_re": ssm_b_re, "ssm_b_im": ssm_b_im,
            "ssm_c_re": ssm_c_re, "ssm_c_im": ssm_c_im, "ssm_d": ssm_d, "w_glu": w_glu,
            "b_glu": b_glu, "w_ssm_o": w_ssm_o, "w_out": w_out, "w_router": w_router,
            "w_e_gate": w_e_gate, "w_e_up": w_e_up, "w_e_down": w_e_down}


def reference(x, c, ctx, c_ctx, w_ada, b_ada, norm1_g, norm2_g, w_in, q_a_g, w_qb, kv_a_g, w_kvb,
              q_norm_g, k_norm_g, w_mla_o, ssm_lam_re, ssm_lam_im, ssm_log_dt, ssm_b_re, ssm_b_im,
              ssm_c_re, ssm_c_im, ssm_d, w_glu, b_glu, w_ssm_o, w_out, w_router, w_e_gate, w_e_up,
              w_e_down):
    n = x.shape[1]
    rope = axial_rope(n)
    for i in range(DEPTH):
        last = i == DEPTH - 1
        mod_x = (jax.nn.silu(c) @ w_ada[i] + b_ada[i])[:, None, :]
        mod_c = (jax.nn.silu(c_ctx) @ w_ada[i] + b_ada[i])[None, None, :]
        sh1x, sc1x, g1x, sh2x, sc2x, g2x = jnp.split(mod_x, 6, axis=-1)
        sh1c, sc1c, g1c, sh2c, sc2c, g2c = jnp.split(mod_c, 6, axis=-1)

        hx = modulate(rmsnorm(x, norm1_g[i]), sh1x, sc1x)
        hc = modulate(rmsnorm(ctx, norm1_g[i]), sh1c, sc1c)
        ux, qax, kvax, krx, gatex = jnp.split(hx @ w_in[i], IN_SPLITS, axis=-1)
        uc, qac, kvac, krc, gatec = jnp.split(hc @ w_in[i], IN_SPLITS, axis=-1)

        qx = mla_q(qax, q_a_g[i], w_qb[i], q_norm_g[i], rope)
        kx, vx = mla_kv(kvax, krx, kv_a_g[i], w_kvb[i], k_norm_g[i], rope)
        kc, vc = mla_kv(kvac, krc, kv_a_g[i], w_kvb[i], k_norm_g[i], None)
        attn_x = attend_blocked(qx, jnp.concatenate([kc, kx], axis=1), jnp.concatenate([vc, vx], axis=1)) @ w_mla_o[i]

        ux32 = ux.astype(jnp.float32)
        uc32 = uc.astype(jnp.float32)
        d32 = ssm_d[i].astype(jnp.float32)
        y_x = d32 * ux32
        y_c = d32 * uc32 if not last else None
        for d, rev in enumerate((False, True)):
            lam_bar, b_bar = s5_discretise(ssm_lam_re[i, d], ssm_lam_im[i, d], ssm_log_dt[i, d],
                                           ssm_b_re[i, d], ssm_b_im[i, d])
            c_out = lax.complex(ssm_c_re[i, d].astype(jnp.float32), ssm_c_im[i, d].astype(jnp.float32))
            xs_c = s5_scan(s5_drive(uc32, b_bar), lam_bar, rev)
            h0 = xs_c[:, 0] if rev else xs_c[:, -1]
            xs_x = s5_scan(s5_drive(ux32, b_bar), lam_bar, rev, h0)
            y_x = y_x + s5_readout(xs_x, c_out)
            if not last:
                y_c = y_c + s5_readout(xs_c, c_out)
        ssm_x = s5_branch_out(y_x.astype(x.dtype), w_glu[i], b_glu[i], w_ssm_o[i])
        x_mid = x + g1x * merge_branches(ssm_x, attn_x, gatex, w_out[i])

        if not last:
            qc = mla_q(qac, q_a_g[i], w_qb[i], q_norm_g[i], None)
            attn_c = attend(qc, kc, vc) @ w_mla_o[i]
            ssm_c = s5_branch_out(y_c.astype(ctx.dtype), w_glu[i], b_glu[i], w_ssm_o[i])
            ctx_mid = ctx + g1c * merge_branches(ssm_c, attn_c, gatec, w_out[i])
            hc2 = modulate(rmsnorm(ctx_mid, norm2_g[i]), sh2c, sc2c)
            ctx = ctx_mid + g2c * expert_choice_ffn(hc2, w_router[i], w_e_gate[i], w_e_up[i], w_e_down[i])

        hx2 = modulate(rmsnorm(x_mid, norm2_g[i]), sh2x, sc2x)
        x = x_mid + g2x * expert_choice_ffn(hx2, w_router[i], w_e_gate[i], w_e_up[i], w_e_down[i])
    return x
```

```python
import functools
import math

import jax
import jax.numpy as jnp
from jax import lax
from jax.experimental import pallas as pl
from jax.experimental.pallas import tpu as pltpu

F32 = jnp.float32
BF16 = jnp.bfloat16

LANES = 128
EPS = 1e-6
GRID_W = 64
SSM_WIDTH = 512
SSM_GROUP = 16
SSM_GROUPS = SSM_WIDTH // SSM_GROUP
SSM_STATE = 64
SSM_CHUNK = 8
GROUPS_PER_BLOCK = LANES // SSM_GROUP
MLA_HEADS = 8
QK_NOPE = 64
QK_ROPE = 32
QK_DIM = QK_NOPE + QK_ROPE
V_DIM = 64
Q_LORA = 384
KV_LORA = 256
ROPE_THETA = 10000.0
N_EXPERTS = 16
CAPACITY_FACTOR = 2
U_END = SSM_WIDTH
QA_END = U_END + Q_LORA
KVA_END = QA_END + KV_LORA
KR_END = KVA_END + QK_ROPE
VMEM_LIMIT = 56 * 1024 * 1024
BISECT_STEPS = 160


def _cparams(sem, vmem=None):
    return pltpu.CompilerParams(dimension_semantics=sem, vmem_limit_bytes=vmem)


def _dot(a, b):
    return jnp.dot(a, b, preferred_element_type=F32)


def _dot_nt(a, b):
    return lax.dot_general(a, b, (((1,), (1,)), ((), ())), preferred_element_type=F32)


def _split_bf16(a):
    hi = a.astype(BF16)
    lo = (a - hi.astype(F32)).astype(BF16)
    return hi, lo


def _rms(x, g):
    ms = jnp.mean(x * x, axis=-1, keepdims=True)
    return x * lax.rsqrt(ms + EPS) * g


def _adaln_body(c_ref, w_ref, b_ref, o_ref):
    c = c_ref[...]
    s = c * jax.nn.sigmoid(c)
    s_hi, s_lo = _split_bf16(s)
    w_hi, w_lo = _split_bf16(w_ref[...])
    o_ref[...] = _dot(s_hi, w_hi) + _dot(s_hi, w_lo) + _dot(s_lo, w_hi) + b_ref[...]


def _adaln(cc, w, b):
    rows, d = cc.shape
    n = w.shape[1]
    tn = n // 4
    return pl.pallas_call(
        _adaln_body,
        grid=(n // tn,),
        in_specs=[pl.BlockSpec((rows, d), lambda j: (0, 0)),
                  pl.BlockSpec((d, tn), lambda j: (0, j)),
                  pl.BlockSpec((1, tn), lambda j: (0, j))],
        out_specs=pl.BlockSpec((rows, tn), lambda j: (0, j)),
        out_shape=jax.ShapeDtypeStruct((rows, n), F32),
        compiler_params=_cparams(("arbitrary",), VMEM_LIMIT),
        name="adaln",
    )(cc, w, b)


def _inproj_body(x_ref, mod_ref, g_ref, wa_ref, wkr_ref, wg_ref,
                 u_ref, qa_ref, kva_ref, kr_ref, gate_ref):
    sh = mod_ref[0, 0:1, :]
    sc = mod_ref[0, 1:2, :]
    h = _rms(x_ref[...], g_ref[...]) * (1.0 + sc) + sh
    hb = h.astype(BF16)
    a = _dot(hb, wa_ref[...])
    u_ref[...] = a[:, :U_END]
    qa_ref[...] = a[:, U_END:QA_END].astype(BF16)
    kva_ref[...] = a[:, QA_END:KVA_END].astype(BF16)
    kr_ref[...] = _dot(hb, wkr_ref[...])
    gate_ref[...] = _dot(hb, wg_ref[...]).astype(BF16)


def _inproj(rows, mod3, g, wa, wkr, wg, tm, row_of_tile):
    r, d = rows.shape
    full = lambda shape: pl.BlockSpec(shape, lambda i: (0,) * len(shape))
    tile = lambda w: pl.BlockSpec((tm, w), lambda i: (i, 0))
    return pl.pallas_call(
        _inproj_body,
        grid=(r // tm,),
        in_specs=[tile(d),
                  pl.BlockSpec((1,) + mod3.shape[1:], lambda i: (row_of_tile(i), 0, 0)),
                  full(g.shape), full(wa.shape), full(wkr.shape), full(wg.shape)],
        out_specs=[tile(SSM_WIDTH), tile(Q_LORA), tile(KV_LORA), tile(LANES), tile(wg.shape[1])],
        out_shape=[jax.ShapeDtypeStruct((r, SSM_WIDTH), F32),
                   jax.ShapeDtypeStruct((r, Q_LORA), BF16),
                   jax.ShapeDtypeStruct((r, KV_LORA), BF16),
                   jax.ShapeDtypeStruct((r, LANES), F32),
                   jax.ShapeDtypeStruct((r, wg.shape[1]), BF16)],
        compiler_params=_cparams(("arbitrary",), VMEM_LIMIT),
        name="inproj",
    )(rows, mod3, g, wa, wkr, wg)


def _swap_pairs(x):
    lane = lax.broadcasted_iota(jnp.int32, x.shape, 1)
    return jnp.where((lane & 15) < 8, pltpu.roll(x, LANES - 8, 1), pltpu.roll(x, 8, 1))


def _norm_rope_head(v, a_tab, b_tab):
    ss = jnp.sum(v * v, axis=-1, keepdims=True)
    rinv = lax.rsqrt(ss * (1.0 / QK_DIM) + EPS)
    return (v * a_tab + _swap_pairs(v) * b_tab) * rinv


def _mla_q_body(qa_ref, g_ref, w_ref, a_ref, b_ref, q_ref):
    h = _rms(qa_ref[...].astype(F32), g_ref[...]).astype(BF16)
    q = _dot(h, w_ref[...])
    a_tab = a_ref[...]
    b_tab = b_ref[...]
    for hd in range(MLA_HEADS):
        v = q[:, hd * LANES:(hd + 1) * LANES]
        q_ref[0, hd] = _norm_rope_head(v, a_tab, b_tab).astype(BF16)


def _mla_q(qa, g, w, a_tab, b_tab, batch, n, tm):
    nt = n // tm
    return pl.pallas_call(
        _mla_q_body,
        grid=(batch, nt),
        in_specs=[pl.BlockSpec((tm, Q_LORA), lambda b, t: (b * nt + t, 0)),
                  pl.BlockSpec(g.shape, lambda b, t: (0, 0)),
                  pl.BlockSpec(w.shape, lambda b, t: (0, 0)),
                  pl.BlockSpec((tm, LANES), lambda b, t: (t, 0)),
                  pl.BlockSpec((tm, LANES), lambda b, t: (t, 0))],
        out_specs=pl.BlockSpec((1, MLA_HEADS, tm, LANES), lambda b, t: (b, 0, t, 0)),
        out_shape=jax.ShapeDtypeStruct((batch, MLA_HEADS, n, LANES), BF16),
        compiler_params=_cparams(("arbitrary", "arbitrary"), VMEM_LIMIT),
        name="mla_q",
    )(qa, g, w, a_tab, b_tab)


def _mla_kv_body(kva_ref, kr_ref, g_ref, wk_ref, wv_ref, a_ref, b_ref, k_ref, v_ref):
    h = _rms(kva_ref[...].astype(F32), g_ref[...]).astype(BF16)
    k = _dot(h, wk_ref[...])
    vv = _dot(h, wv_ref[...])
    kr = kr_ref[...]
    a_tab = a_ref[...]
    b_tab = b_ref[...]
    for hd in range(MLA_HEADS):
        kh = k[:, hd * LANES:(hd + 1) * LANES] + kr
        k_ref[0, hd] = _norm_rope_head(kh, a_tab, b_tab).astype(BF16)
        v_ref[0, hd] = vv[:, hd * LANES:(hd + 1) * LANES].astype(BF16)


def _mla_kv(kva, kr, g, wk, wv, a_tab, b_tab, batch, n, tm):
    nt = n // tm
    out = jax.ShapeDtypeStruct((batch, MLA_HEADS, n, LANES), BF16)
    ospec = pl.BlockSpec((1, MLA_HEADS, tm, LANES), lambda b, t: (b, 0, t, 0))
    return pl.pallas_call(
        _mla_kv_body,
        grid=(batch, nt),
        in_specs=[pl.BlockSpec((tm, KV_LORA), lambda b, t: (b * nt + t, 0)),
                  pl.BlockSpec((tm, LANES), lambda b, t: (b * nt + t, 0)),
                  pl.BlockSpec(g.shape, lambda b, t: (0, 0)),
                  pl.BlockSpec(wk.shape, lambda b, t: (0, 0)),
                  pl.BlockSpec(wv.shape, lambda b, t: (0, 0)),
                  pl.BlockSpec((tm, LANES), lambda b, t: (t, 0)),
                  pl.BlockSpec((tm, LANES), lambda b, t: (t, 0))],
        out_specs=[ospec, ospec],
        out_shape=[out, out],
        compiler_params=_cparams(("arbitrary", "arbitrary"), VMEM_LIMIT),
        name="mla_kv",
    )(kva, kr, g, wk, wv, a_tab, b_tab)


def _attn_body(q_ref, kc_ref, vc_ref, kx_ref, vx_ref, o_ref):
    out = None
    for hh in range(2):
        q = q_ref[0, hh]
        s_c = _dot_nt(q, kc_ref[0, hh])
        s_x = _dot_nt(q, kx_ref[0, hh])
        m = jnp.maximum(jnp.max(s_c, axis=-1, keepdims=True), jnp.max(s_x, axis=-1, keepdims=True))
        p_c = jnp.exp(s_c - m)
        p_x = jnp.exp(s_x - m)
        l = jnp.sum(p_c, axis=-1, keepdims=True) + jnp.sum(p_x, axis=-1, keepdims=True)
        o = _dot(p_c.astype(BF16), vc_ref[0, hh]) + _dot(p_x.astype(BF16), vx_ref[0, hh])
        o = o / l
        out = o if hh == 0 else out + pltpu.roll(o, V_DIM, 1)
    o_ref[0] = out.astype(BF16)


def _attn(q, kc, vc, kx, vx, tq):
    batch, heads, n, _ = q.shape
    nc = kc.shape[2]
    hp = heads // 2
    kv_spec = lambda rows: pl.BlockSpec((1, 2, rows, LANES), lambda b, h, t: (b, h, 0, 0))
    return pl.pallas_call(
        _attn_body,
        grid=(batch, hp, n // tq),
        in_specs=[pl.BlockSpec((1, 2, tq, LANES), lambda b, h, t: (b, h, t, 0)),
                  kv_spec(nc), kv_spec(nc), kv_spec(n), kv_spec(n)],
        out_specs=pl.BlockSpec((1, tq, LANES), lambda b, h, t: (b, t, h)),
        out_shape=jax.ShapeDtypeStruct((batch, n, hp * LANES), BF16),
        compiler_params=_cparams(("arbitrary",) * 3, VMEM_LIMIT),
        name="attn",
    )(q, kc, vc, kx, vx)


def _s5_operators(lam_re, lam_im, log_dt, b_re, b_im, c_re, c_im):
    t_len = SSM_CHUNK
    hp = lax.Precision.HIGHEST
    dt = jnp.exp(log_dt)[..., None]
    ar, ai = lam_re * dt, lam_im * dt
    k = jnp.arange(t_len + 1, dtype=F32)
    mag = jnp.exp(ar[..., None] * k)
    ang = ai[..., None] * k
    pw_re, pw_im = mag * jnp.cos(ang), mag * jnp.sin(ang)
    lb_re, lb_im = pw_re[..., 1], pw_im[..., 1]
    den = lam_re * lam_re + lam_im * lam_im
    q_re = ((lb_re - 1.0) * lam_re + lb_im * lam_im) / den
    q_im = (lb_im * lam_re - (lb_re - 1.0) * lam_im) / den
    bb_re = q_re[..., None] * b_re - q_im[..., None] * b_im
    bb_im = q_re[..., None] * b_im + q_im[..., None] * b_re
    cp_re = c_re[..., None] * pw_re[:, :, None] - c_im[..., None] * pw_im[:, :, None]
    cp_im = c_re[..., None] * pw_im[:, :, None] + c_im[..., None] * pw_re[:, :, None]
    kern = (jnp.einsum("dgipk,dgpj->dgkij", cp_re, bb_re, precision=hp)
            - jnp.einsum("dgipk,dgpj->dgkij", cp_im, bb_im, precision=hp))
    s = jnp.arange(t_len)
    lag_f = s[None, :] - s[:, None]
    nb, gl = SSM_GROUPS // GROUPS_PER_BLOCK, GROUPS_PER_BLOCK
    eye = jnp.eye(gl, dtype=F32)

    def per_dir(d, lag, e_idx, f_idx):
        valid = (lag >= 0).astype(F32)
        kst = kern[d][:, jnp.clip(lag, 0, t_len)] * valid[None, :, :, None, None]
        kst = kst.reshape(nb, gl, t_len, t_len, SSM_GROUP, SSM_GROUP)
        intra = jnp.einsum("lgstij,gh->lsgjthi", kst, eye)
        intra = intra.reshape(nb, t_len * LANES, t_len * LANES)
        w_re = pw_re[d][:, :, e_idx][..., None] * bb_re[d][:, :, None] - pw_im[d][:, :, e_idx][..., None] * bb_im[d][:, :, None]
        w_im = pw_re[d][:, :, e_idx][..., None] * bb_im[d][:, :, None] + pw_im[d][:, :, e_idx][..., None] * bb_re[d][:, :, None]
        w = jnp.stack([w_re, w_im], axis=0)
        w = w.reshape(2, nb, gl, SSM_STATE, t_len, SSM_GROUP)
        u2h = jnp.einsum("rlgpsj,gh->lsgjrhp", w, eye).reshape(nb, t_len * LANES, 2 * gl * SSM_STATE)
        v = jnp.stack([cp_re[d][..., f_idx], -cp_im[d][..., f_idx]], axis=0)
        v = v.reshape(2, nb, gl, SSM_GROUP, SSM_STATE, t_len)
        h2y = jnp.einsum("rlgipt,gh->lrgpthi", v, eye).reshape(nb, 2 * gl * SSM_STATE, t_len * LANES)
        lam_t = jnp.stack([pw_re[d][..., t_len], pw_im[d][..., t_len]], axis=0)
        lam_t = lam_t.reshape(2, nb, gl * SSM_STATE).transpose(1, 0, 2)
        return intra, u2h, h2y, lam_t

    fwd = per_dir(0, lag_f, t_len - 1 - s, s + 1)
    bwd = per_dir(1, -lag_f, s, t_len - s)
    stack = lambda i, dt_: jnp.stack([fwd[i], bwd[i]], axis=0).astype(dt_)
    return stack(0, BF16), stack(1, BF16), stack(2, BF16), stack(3, F32)


def _s5_body(ux_ref, uc_ref, d_ref, intra_ref, u2h_ref, h2y_ref, lam_ref, y_ref,
             ux_scr, uc_scr, sx_scr, sc_scr, hin_scr):
    d = pl.program_id(2)
    nx = ux_scr.shape[0]
    nc = uc_scr.shape[0]
    half = lam_ref.shape[-1]
    t_len = SSM_CHUNK

    @pl.when(d == 0)
    def _():
        for s in range(t_len):
            ux_scr[:, s * LANES:(s + 1) * LANES] = ux_ref[0, pl.ds(s, nx, stride=t_len), :].astype(BF16)
            uc_scr[:, s * LANES:(s + 1) * LANES] = uc_ref[0, pl.ds(s, nc, stride=t_len), :].astype(BF16)

    u2h = u2h_ref[0, 0]
    sx_scr[...] = _dot(ux_scr[...], u2h)
    sc_scr[...] = _dot(uc_scr[...], u2h)
    lam_re = lam_ref[0, 0, 0:1, :]
    lam_im = lam_ref[0, 0, 1:2, :]

    def step(src_ref, idx, carry, store):
        h_re, h_im = carry
        if store:
            hin_scr[pl.ds(idx, 1), 0:half] = h_re
            hin_scr[pl.ds(idx, 1), half:2 * half] = h_im
        row = src_ref[pl.ds(idx, 1), :]
        n_re = lam_re * h_re - lam_im * h_im + row[:, 0:half]
        n_im = lam_re * h_im + lam_im * h_re + row[:, half:2 * half]
        return n_re, n_im

    zero = jnp.zeros((1, half), F32)
    carry = lax.fori_loop(
        0, nc, lambda i, c: step(sc_scr, jnp.where(d == 0, i, nc - 1 - i), c, False), (zero, zero))
    lax.fori_loop(
        0, nx, lambda i, c: step(sx_scr, jnp.where(d == 0, i, nx - 1 - i), c, True), carry)

    y = _dot(ux_scr[...], intra_ref[0, 0]) + _dot(hin_scr[...].astype(BF16), h2y_ref[0, 0])

    @pl.when(d == 0)
    def _():
        dd = d_ref[...]
        for t in range(t_len):
            rows = pl.ds(t, nx, stride=t_len)
            y_ref[0, rows, :] = y[:, t * LANES:(t + 1) * LANES] + dd * ux_ref[0, rows, :]

    @pl.when(d != 0)
    def _():
        for t in range(t_len):
            rows = pl.ds(t, nx, stride=t_len)
            y_ref[0, rows, :] = y_ref[0, rows, :] + y[:, t * LANES:(t + 1) * LANES]


def _s5(ux, uc, ssm_d, intra, u2h, h2y, lam_t):
    batch, n, width = ux.shape
    nctx = uc.shape[1]
    nb = width // LANES
    kk = SSM_CHUNK * LANES
    st = u2h.shape[-1]
    op_spec = lambda a: pl.BlockSpec((1, 1) + a.shape[2:], lambda b, l, d: (d, l, 0, 0))
    return pl.pallas_call(
        _s5_body,
        grid=(batch, nb, 2),
        in_specs=[pl.BlockSpec((1, n, LANES), lambda b, l, d: (b, 0, l)),
                  pl.BlockSpec((1, nctx, LANES), lambda b, l, d: (b, 0, l)),
                  pl.BlockSpec((1, LANES), lambda b, l, d: (0, l)),
                  op_spec(intra), op_spec(u2h), op_spec(h2y), op_spec(lam_t)],
        out_specs=pl.BlockSpec((1, n, LANES), lambda b, l, d: (b, 0, l)),
        out_shape=jax.ShapeDtypeStruct((batch, n, width), F32),
        scratch_shapes=[pltpu.VMEM((n // SSM_CHUNK, kk), BF16),
                        pltpu.VMEM((nctx // SSM_CHUNK, kk), BF16),
                        pltpu.VMEM((n // SSM_CHUNK, st), F32),
                        pltpu.VMEM((nctx // SSM_CHUNK, st), F32),
                        pltpu.VMEM((n // SSM_CHUNK, st), F32)],
        compiler_params=_cparams(("arbitrary",) * 3, VMEM_LIMIT),
        name="s5",
    )(ux, uc, ssm_d, intra, u2h, h2y, lam_t)


def _merge_body(x_ref, y_ref, attn_ref, gate_ref, mod_ref, wglu_ref, bglu_ref, wso_ref,
                wmo_ref, wout_ref, g2_ref, wr_ref, xmid_ref, h2_ref, aff_ref):
    d = x_ref.shape[-1]
    yg = jax.nn.gelu(y_ref[...])
    z = _dot(yg.astype(BF16), wglu_ref[...]) + bglu_ref[...]
    yy = yg * jax.nn.sigmoid(z)
    ssm = _dot(yy.astype(BF16), wso_ref[...])
    att = _dot(attn_ref[...], wmo_ref[...])
    gates = gate_ref[...].astype(F32)
    mrg = jax.nn.sigmoid(gates[:, :d]) * ssm + jax.nn.sigmoid(gates[:, d:]) * att
    o = _dot(mrg.astype(BF16), wout_ref[...])
    g1 = mod_ref[0, 2:3, :]
    sh2 = mod_ref[0, 3:4, :]
    sc2 = mod_ref[0, 4:5, :]
    x_mid = x_ref[...] + g1 * o
    xmid_ref[...] = x_mid
    h2 = _rms(x_mid, g2_ref[...]) * (1.0 + sc2) + sh2
    h2_ref[...] = h2.astype(BF16)
    h_hi, h_lo = _split_bf16(h2)
    wr = wr_ref[...]
    r = _dot_nt(wr, h_hi) + _dot_nt(wr, h_lo)
    logits = r[:N_EXPERTS] + r[N_EXPERTS:]
    m = jnp.max(logits, axis=0, keepdims=True)
    e = jnp.exp(logits - m)
    aff_ref[0] = e / jnp.sum(e, axis=0, keepdims=True)


def _merge(x, y, attn, gates, mod3, wglu, bglu, wso, wmo, wout, g2, wr, batch, n, tm):
    r, d = x.shape
    nt = n // tm
    full = lambda a: pl.BlockSpec(a.shape, lambda i: (0,) * a.ndim)
    tile = lambda w: pl.BlockSpec((tm, w), lambda i: (i, 0))
    return pl.pallas_call(
        _merge_body,
        grid=(r // tm,),
        in_specs=[tile(d), tile(SSM_WIDTH), tile(attn.shape[1]), tile(gates.shape[1]),
                  pl.BlockSpec((1,) + mod3.shape[1:], lambda i: (i // nt, 0, 0)),
                  full(wglu), full(bglu), full(wso), full(wmo), full(wout), full(g2), full(wr)],
        out_specs=[tile(d), tile(d),
                   pl.BlockSpec((1, N_EXPERTS, tm), lambda i: (i // nt, 0, i % nt))],
        out_shape=[jax.ShapeDtypeStruct((r, d), F32),
                   jax.ShapeDtypeStruct((r, d), BF16),
                   jax.ShapeDtypeStruct((batch, N_EXPERTS, n), F32)],
        compiler_params=_cparams(("arbitrary",), VMEM_LIMIT),
        name="merge",
    )(x, y, attn, gates, mod3, wglu, bglu, wso, wmo, wout, g2, wr)


def _route_body(aff_ref, pos_ref, cum_scr, *, cap):
    aff = aff_ref[...]
    ne, n = aff.shape

    def search(_, lohi):
        lo, hi = lohi
        mid = 0.5 * (lo + hi)
        cnt = jnp.sum(jnp.where(aff >= mid, 1.0, 0.0).astype(F32), axis=1, keepdims=True)
        ok = cnt >= cap
        return jnp.where(ok, mid, lo), jnp.where(ok, hi, mid)

    lo0 = jnp.zeros((ne, 1), F32)
    hi0 = jnp.full((ne, 1), 2.0, F32)
    lo, _ = lax.fori_loop(0, BISECT_STEPS, search, (lo0, hi0))
    thr = jnp.min(jnp.where(aff >= lo, aff, 4.0), axis=1, keepdims=True)
    gt = aff > thr
    eq = aff == thr
    gt_f = jnp.where(gt, 1.0, 0.0).astype(F32)
    eq_f = jnp.where(eq, 1.0, 0.0).astype(F32)
    need = cap - jnp.sum(gt_f, axis=1, keepdims=True)
    ind_f = jnp.concatenate([gt_f, eq_f], axis=0)
    tri = (lax.broadcasted_iota(jnp.int32, (LANES, LANES), 0)
           <= lax.broadcasted_iota(jnp.int32, (LANES, LANES), 1)).astype(BF16)
    carry = jnp.zeros((2 * ne, 1), F32)
    for c in range(n // LANES):
        blk = ind_f[:, c * LANES:(c + 1) * LANES]
        incl = _dot(blk.astype(BF16), tri) + carry
        cum_scr[:, c * LANES:(c + 1) * LANES] = incl - blk
        carry = incl[:, LANES - 1:LANES]
    excl = cum_scr[...]
    gt_before = excl[:ne]
    eq_before = excl[ne:]
    sel = gt | (eq & (eq_before < need))
    pos = gt_before + jnp.minimum(eq_before, need)
    pos_ref[...] = jnp.where(sel, pos.astype(jnp.int32), -1)


def _route(aff, cap):
    batch, ne, n = aff.shape
    rows = batch * ne
    pos = pl.pallas_call(
        functools.partial(_route_body, cap=cap),
        grid=(1,),
        in_specs=[pl.BlockSpec((rows, n), lambda i: (0, 0))],
        out_specs=pl.BlockSpec((rows, n), lambda i: (0, 0)),
        out_shape=jax.ShapeDtypeStruct((rows, n), jnp.int32),
        scratch_shapes=[pltpu.VMEM((2 * rows, n), F32)],
        compiler_params=_cparams(("arbitrary",), VMEM_LIMIT),
        name="route",
    )(aff.reshape(rows, n))
    return pos.reshape(batch, ne, n)


def _gather_body(pos_ref, h_ref, xs_ref, *, cap, tb):
    e = pl.program_id(1)
    n = h_ref.shape[1]
    prow = pos_ref[0, pl.ds(e, 1), :]
    slot = lax.broadcasted_iota(jnp.int32, (cap, tb), 0)
    acc = jnp.zeros((cap, h_ref.shape[2]), F32)
    for t in range(n // tb):
        onehot = jnp.where(prow[:, t * tb:(t + 1) * tb] == slot, 1.0, 0.0).astype(BF16)
        acc = acc + _dot(onehot, h_ref[0, t * tb:(t + 1) * tb, :])
    xs_ref[0] = acc.astype(BF16)


def _gather(pos, h, cap, tb):
    batch, ne, n = pos.shape
    d = h.shape[-1]
    return pl.pallas_call(
        functools.partial(_gather_body, cap=cap, tb=tb),
        grid=(batch, ne),
        in_specs=[pl.BlockSpec((1, ne, n), lambda b, e: (b, 0, 0)),
                  pl.BlockSpec((1, n, d), lambda b, e: (b, 0, 0))],
        out_specs=pl.BlockSpec((1, cap, d), lambda b, e: (e, b, 0)),
        out_shape=jax.ShapeDtypeStruct((ne, batch * cap, d), BF16),
        compiler_params=_cparams(("arbitrary", "arbitrary"), VMEM_LIMIT),
        name="gather",
    )(pos, h)


def _experts_body(xs_ref, wg_ref, wu_ref, wd_ref, ys_ref, acc_scr):
    f = pl.program_id(1)
    x = xs_ref[0]
    g = _dot(x, wg_ref[0].astype(BF16))
    u = _dot(x, wu_ref[0].astype(BF16))
    hid = (g * jax.nn.sigmoid(g) * u).astype(BF16)
    part = _dot(hid, wd_ref[0].astype(BF16))

    @pl.when(f == 0)
    def _():
        acc_scr[...] = part

    @pl.when(f != 0)
    def _():
        acc_scr[...] = acc_scr[...] + part

    @pl.when(f == pl.num_programs(1) - 1)
    def _():
        ys_ref[0] = acc_scr[...].astype(BF16)


def _experts(xs, wg, wu, wd, tf):
    ne, rows, d = xs.shape
    ff = wg.shape[-1]
    return pl.pallas_call(
        _experts_body,
        grid=(ne, ff // tf),
        in_specs=[pl.BlockSpec((1, rows, d), lambda e, f: (e, 0, 0)),
                  pl.BlockSpec((1, d, tf), lambda e, f: (e, 0, f)),
                  pl.BlockSpec((1, d, tf), lambda e, f: (e, 0, f)),
                  pl.BlockSpec((1, tf, d), lambda e, f: (e, f, 0))],
        out_specs=pl.BlockSpec((1, rows, d), lambda e, f: (e, 0, 0)),
        out_shape=jax.ShapeDtypeStruct((ne, rows, d), BF16),
        scratch_shapes=[pltpu.VMEM((rows, d), F32)],
        compiler_params=_cparams(("arbitrary", "arbitrary"), VMEM_LIMIT),
        name="experts",
    )(xs, wg, wu, wd)


def _combine_body(xmid_ref, pos_ref, gate_ref, ys_ref, mod_ref, o_ref, *, cap):
    tb = xmid_ref.shape[0]
    slot = lax.broadcasted_iota(jnp.int32, (tb, cap), 1)
    pos = pos_ref[0]
    gate = gate_ref[0]
    acc = jnp.zeros(xmid_ref.shape, F32)
    for e in range(N_EXPERTS):
        w = jnp.where(pos[:, e:e + 1] == slot, gate[:, e:e + 1], 0.0).astype(BF16)
        acc = acc + _dot(w, ys_ref[e])
    o_ref[...] = xmid_ref[...] + mod_ref[0, 5:6, :] * acc


def _combine(xmid, pos_t, gate_t, ys, mod3, cap, tb):
    r, d = xmid.shape
    batch, n, ne = pos_t.shape
    nt = n // tb
    return pl.pallas_call(
        functools.partial(_combine_body, cap=cap),
        grid=(batch, nt),
        in_specs=[pl.BlockSpec((tb, d), lambda b, t: (b * nt + t, 0)),
                  pl.BlockSpec((1, tb, ne), lambda b, t: (b, t, 0)),
                  pl.BlockSpec((1, tb, ne), lambda b, t: (b, t, 0)),
                  pl.BlockSpec((ne, cap, d), lambda b, t: (0, b, 0), pipeline_mode=pl.Buffered(1)),
                  pl.BlockSpec((1,) + mod3.shape[1:], lambda b, t: (b, 0, 0))],
        out_specs=pl.BlockSpec((tb, d), lambda b, t: (b * nt + t, 0)),
        out_shape=jax.ShapeDtypeStruct((r, d), F32),
        compiler_params=_cparams(("arbitrary", "arbitrary"), VMEM_LIMIT),
        name="combine",
    )(xmid, pos_t, gate_t, ys, mod3)


def _rope_tables(n, gain, scale):
    rows = n // GRID_W
    row = jnp.repeat(jnp.arange(rows, dtype=F32), GRID_W)
    col = jnp.tile(jnp.arange(GRID_W, dtype=F32), rows)
    pairs = QK_ROPE // 4
    inv_freq = ROPE_THETA ** (-jnp.arange(pairs, dtype=F32) / pairs)
    ar, ac = row[:, None] * inv_freq, col[:, None] * inv_freq
    cos32 = jnp.concatenate([jnp.cos(ar), jnp.cos(ar), jnp.cos(ac), jnp.cos(ac)], axis=1)
    sin32 = jnp.concatenate([-jnp.sin(ar), jnp.sin(ar), -jnp.sin(ac), jnp.sin(ac)], axis=1)
    return _head_tables(gain, cos32, sin32, scale)


def _head_tables(gain, cos32, sin32, scale):
    n = cos32.shape[0]
    g_nope, g_rope = gain[:QK_NOPE], gain[QK_NOPE:]
    g_swap = g_rope.reshape(2, 2, QK_ROPE // 4)[:, ::-1].reshape(QK_ROPE)
    pad = jnp.zeros((n, LANES - QK_DIM), F32)
    a = jnp.concatenate([jnp.broadcast_to(g_nope, (n, QK_NOPE)), g_rope * cos32, pad], axis=1)
    b = jnp.concatenate([jnp.zeros((n, QK_NOPE), F32), g_swap * sin32, pad], axis=1)
    return a * scale, b * scale


def _pad_heads(w, width, offset=0):
    k = w.shape[0]
    w = w.reshape(k, MLA_HEADS, width)
    w = jnp.pad(w, ((0, 0), (0, 0), (offset, LANES - width - offset)))
    return w.reshape(k, MLA_HEADS * LANES)


def _pick_tile(n, want):
    t = min(n, want)
    while n % t:
        t //= 2
    return t


def kernel(x, c, ctx, c_ctx, w_ada, b_ada, norm1_g, norm2_g, w_in, q_a_g, w_qb, kv_a_g, w_kvb,
           q_norm_g, k_norm_g, w_mla_o, ssm_lam_re, ssm_lam_im, ssm_log_dt, ssm_b_re, ssm_b_im,
           ssm_c_re, ssm_c_im, ssm_d, w_glu, b_glu, w_ssm_o, w_out, w_router, w_e_gate, w_e_up,
           w_e_down):
    batch, n, d = x.shape
    nctx = ctx.shape[1]
    assert w_ada.shape[0] == 1, "single-layer block"
    assert n % (SSM_CHUNK * LANES) == 0 and nctx % (SSM_CHUNK * 16) == 0 and n % GRID_W == 0
    cap = CAPACITY_FACTOR * n // N_EXPERTS

    cc = jnp.concatenate([c, c_ctx[None], jnp.zeros((8 - batch - 1, d), F32)], axis=0)
    mod3 = _adaln(cc, w_ada[0], b_ada[0][None]).reshape(8, 6, d)

    w_in0 = w_in[0]
    wa = w_in0[:, :KVA_END].astype(BF16)
    wkr = jnp.pad(w_in0[:, KVA_END:KR_END], ((0, 0), (QK_NOPE, LANES - QK_DIM))).astype(BF16)
    wg = w_in0[:, KR_END:].astype(BF16)
    g1n = norm1_g[0][None]
    tm = _pick_tile(n, 512)
    ntile = n // tm
    ux, qax, kvax, krx, gatex = _inproj(x.reshape(batch * n, d), mod3, g1n, wa, wkr, wg, tm,
                                        lambda i: i // ntile)
    tmc = _pick_tile(batch * nctx, 512)
    uc, _, kvac, krc, _ = _inproj(ctx.reshape(batch * nctx, d), mod3, g1n, wa, wkr, wg, tmc,
                                  lambda i: batch)

    wq = _pad_heads(w_qb[0], QK_DIM).astype(BF16)
    kvb = w_kvb[0].reshape(KV_LORA, MLA_HEADS, QK_NOPE + V_DIM)
    wk = _pad_heads(kvb[:, :, :QK_NOPE].reshape(KV_LORA, -1), QK_NOPE).astype(BF16)
    wv = _pad_heads(kvb[:, :, QK_NOPE:].reshape(KV_LORA, -1), V_DIM).astype(BF16)
    aq, bq = _rope_tables(n, q_norm_g[0], QK_DIM ** -0.5)
    ak, bk = _rope_tables(n, k_norm_g[0], 1.0)
    akc, bkc = _head_tables(k_norm_g[0], jnp.ones((nctx, QK_ROPE), F32),
                            jnp.zeros((nctx, QK_ROPE), F32), 1.0)
    tq = _pick_tile(n, 512)
    q = _mla_q(qax, q_a_g[0][None], wq, aq, bq, batch, n, tq)
    kx, vx = _mla_kv(kvax, krx, kv_a_g[0][None], wk, wv, ak, bk, batch, n, tq)
    kc, vc = _mla_kv(kvac, krc, kv_a_g[0][None], wk, wv, akc, bkc, batch, nctx, _pick_tile(nctx, 256))
    attn = _attn(q, kc, vc, kx, vx, _pick_tile(n, 256))

    ops = _s5_operators(ssm_lam_re[0], ssm_lam_im[0], ssm_log_dt[0], ssm_b_re[0], ssm_b_im[0],
                        ssm_c_re[0], ssm_c_im[0])
    y = _s5(ux.reshape(batch, n, SSM_WIDTH), uc.reshape(batch, nctx, SSM_WIDTH), ssm_d, *ops)

    wr_hi, wr_lo = _split_bf16(w_router[0].T)
    wr = jnp.concatenate([wr_hi, wr_lo], axis=0)
    x_mid, h2, aff = _merge(
        x.reshape(batch * n, d), y.reshape(batch * n, SSM_WIDTH), attn.reshape(batch * n, -1), gatex,
        mod3, w_glu[0].astype(BF16), b_glu[0][None], w_ssm_o[0].astype(BF16),
        w_mla_o[0].astype(BF16), w_out[0].astype(BF16), norm2_g[0][None], wr, batch, n,
        _pick_tile(n, 256))

    pos = _route(aff, cap)
    xs = _gather(pos, h2.reshape(batch, n, d), cap, _pick_tile(n, 512))
    ff = w_e_gate.shape[-1]
    ys = _experts(xs, w_e_gate[0], w_e_up[0], w_e_down[0], 256 if ff % 256 == 0 else ff)
    out = _combine(x_mid, pos.transpose(0, 2, 1), aff.transpose(0, 2, 1), ys, mod3, cap,
                   _pick_tile(n, 512))
    return out.reshape(batch, n, d)
```
